```python
import math
import jax, jax.numpy as jnp
from jax import lax
import numpy as np

D_MODEL = 2048
BATCH = 4
SEQ = 4096
DEPTH = 2

CHUNK = 64
D_MIX = D_MODEL
A_HEADS = 8
A_HEAD_DIM = D_MODEL // 16
A_WIDTH = A_HEADS * A_HEAD_DIM
CONV_WIDTH = 4
B_HEADS = 8
B_KV_HEADS = 2
B_HEAD_DIM = D_MODEL // 32
B_WIDTH = B_HEADS * B_HEAD_DIM
B_KV_DIM = B_KV_HEADS * B_HEAD_DIM
WINDOW = 128
WINDOW_CHUNKS = WINDOW // CHUNK
C_WIDTH = D_MIX - A_WIDTH - B_WIDTH
S5_GROUP = 16
S5_GROUPS = C_WIDTH // S5_GROUP
S5_STATE = 64
D_IN = 3 * A_WIDTH + A_WIDTH + 2 * A_HEADS + B_WIDTH + 2 * B_KV_DIM + C_WIDTH
D_FF = 7 * D_MODEL // 2
N_EXPERTS = 8
TOP_K = 2
EPS = 1e-6

kernel_name = "hybrid_parallel_deltanet_swa_s5_moe"


def _split_points():
    sizes = (3 * A_WIDTH, A_WIDTH, A_HEADS, A_HEADS, B_WIDTH, B_KV_DIM, B_KV_DIM, C_WIDTH)
    pts, acc = [], 0
    for s in sizes[:-1]:
        acc += s
        pts.append(acc)
    return pts


def rms_norm(x, gain):
    xf = x.astype(jnp.float32)
    y = xf * lax.rsqrt(jnp.mean(xf * xf, axis=-1, keepdims=True) + EPS)
    return (y * gain.astype(jnp.float32)).astype(x.dtype)


def l2_norm(x):
    return x * lax.rsqrt(jnp.sum(x * x, axis=-1, keepdims=True) + EPS)


def causal_depthwise_conv(x, w):
    k_taps = w.shape[0]
    l = x.shape[1]
    xp = jnp.pad(x, ((0, 0), (k_taps - 1, 0), (0, 0)))
    y = xp[:, 0:l] * w[0]
    for j in range(1, k_taps):
        y = y + xp[:, j:j + l] * w[j]
    return y


def gated_delta_rule(q, k, v, g, beta):
    b, l, h, dk = q.shape
    dv = v.shape[-1]
    nc = l // CHUNK

    def to_chunks(t):
        t = t.reshape((b, nc, CHUNK, h) + t.shape[3:])
        return jnp.moveaxis(t, 3, 1)

    q, k, v, g, beta = (to_chunks(t) for t in (q, k, v, g, beta))
    gc = jnp.cumsum(g, axis=-1)
    idx = jnp.arange(CHUNK)
    incl = idx[:, None] >= idx[None, :]
    strict = idx[:, None] > idx[None, :]
    gdiff = gc[..., :, None] - gc[..., None, :]
    decay_incl = jnp.exp(jnp.where(incl, gdiff, -jnp.inf))
    decay_strict = jnp.where(strict, decay_incl, 0.0)
    kb = k * beta[..., None]
    lower = jnp.einsum('bhnid,bhnjd->bhnij', kb, k) * decay_strict
    eye = jnp.eye(CHUNK, dtype=q.dtype)
    rhs = jnp.concatenate([v * beta[..., None], kb * jnp.exp(gc)[..., None]], axis=-1)
    sol = lax.linalg.triangular_solve(lower + eye, rhs, left_side=True, lower=True, unit_diagonal=True)
    u, w = sol[..., :dv], sol[..., dv:]
    attn = jnp.einsum('bhnid,bhnjd->bhnij', q, k) * decay_incl
    g_last = gc[..., -1]
    q_dec = q * jnp.exp(gc)[..., None]
    k_dec = k * jnp.exp(g_last[..., None] - gc)[..., None]
    xs = tuple(jnp.moveaxis(t, 2, 0) for t in (q_dec, k_dec, u, w, attn, g_last))

    def step(state, inp):
        qd, kd, uc, wc, ac, gl = inp
        v_new = uc - jnp.einsum('bhcd,bhde->bhce', wc, state)
        o = jnp.einsum('bhcd,bhde->bhce', qd, state) + jnp.einsum('bhij,bhje->bhie', ac, v_new)
        state = state * jnp.exp(gl)[..., None, None] + jnp.einsum('bhcd,bhce->bhde', kd, v_new)
        return state, o

    s0 = jnp.zeros((b, h, dk, dv), q.dtype)
    _, o = lax.scan(step, s0, xs)
    o = jnp.transpose(o, (1, 0, 3, 2, 4))
    return o.reshape(b, l, h, dv)


def mixer_deltanet(qkv, gate, a_logit, b_logit, conv_w, a_log, dt_bias, norm_gain):
    b, l, _ = qkv.shape
    f32 = jnp.float32
    qkv = jax.nn.silu(causal_depthwise_conv(qkv, conv_w))
    q, k, v = jnp.split(qkv, 3, axis=-1)
    shp = (b, l, A_HEADS, A_HEAD_DIM)
    q = l2_norm(q.reshape(shp).astype(f32)) * (A_HEAD_DIM ** -0.5)
    k = l2_norm(k.reshape(shp).astype(f32))
    v = v.reshape(shp).astype(f32)
    g = -jnp.exp(a_log.astype(f32)) * jax.nn.softplus(a_logit.astype(f32) + dt_bias.astype(f32))
    beta = jax.nn.sigmoid(b_logit.astype(f32))
    o = gated_delta_rule(q, k, v, g, beta)
    o = rms_norm(o, norm_gain) * jax.nn.silu(gate.reshape(shp).astype(f32))
    return o.reshape(b, l, A_WIDTH).astype(qkv.dtype)


def mixer_window_attention(q, k, v, sinks):
    b, l, _ = q.shape
    nc = l // CHUNK
    grp = B_HEADS // B_KV_HEADS
    band_len = (WINDOW_CHUNKS + 1) * CHUNK
    q = q.reshape(b, nc, CHUNK, B_KV_HEADS, grp, B_HEAD_DIM)

    def band(t):
        t = t.reshape(b, nc, CHUNK, B_KV_HEADS, B_HEAD_DIM)
        t = jnp.pad(t, ((0, 0), (WINDOW_CHUNKS, 0), (0, 0), (0, 0), (0, 0)))
        return jnp.concatenate([t[:, o:o + nc] for o in range(WINDOW_CHUNKS + 1)], axis=2)

    kb, vb = band(k), band(v)
    s = jnp.einsum('bnqkgd,bnskd->bnkgqs', q, kb).astype(jnp.float32) * (B_HEAD_DIM ** -0.5)
    src_chunk = jnp.arange(nc)[:, None] - WINDOW_CHUNKS + (jnp.arange(band_len) // CHUNK)[None, :]
    valid = src_chunk >= 0
    s = jnp.where(valid[None, :, None, None, None, :], s, -jnp.inf)
    sink = sinks.astype(jnp.float32).reshape(B_KV_HEADS, grp)[None, None, :, :, None, None]
    m = jnp.maximum(jnp.max(s, axis=-1, keepdims=True), sink)
    p = jnp.exp(s - m)
    denom = jnp.sum(p, axis=-1, keepdims=True) + jnp.exp(sink - m)
    o = jnp.einsum('bnkgqs,bnskd->bnqkgd', (p / denom).astype(vb.dtype), vb)
    return o.reshape(b, l, B_WIDTH)


def _complex_affine_combine(e1, e2):
    a1r, a1i, b1r, b1i = e1
    a2r, a2i, b2r, b2i = e2
    ar = a2r * a1r - a2i * a1i
    ai = a2r * a1i + a2i * a1r
    br = a2r * b1r - a2i * b1i + b2r
    bi = a2r * b1i + a2i * b1r + b2i
    return (ar, ai, br, bi)


def mixer_s5(u, lam_re, lam_im, log_dt, s5_b_re, s5_b_im, s5_c_re, s5_c_im, s5_d, w_glu, b_glu):
    f32 = jnp.float32
    bsz, l, _ = u.shape
    uf = u.astype(f32).reshape(bsz, l, S5_GROUPS, S5_GROUP)
    dt = jnp.exp(log_dt.astype(f32))[:, None]
    lr = jnp.minimum(lam_re.astype(f32), -1e-4)
    li = lam_im.astype(f32)
    mag = jnp.exp(lr * dt)
    ab_re = mag * jnp.cos(li * dt)
    ab_im = mag * jnp.sin(li * dt)
    den = lr * lr + li * li
    nr = ab_re - 1.0
    coef_re = (nr * lr + ab_im * li) / den
    coef_im = (ab_im * lr - nr * li) / den
    br, bi = s5_b_re.astype(f32), s5_b_im.astype(f32)
    bb_re = coef_re[..., None] * br - coef_im[..., None] * bi
    bb_im = coef_re[..., None] * bi + coef_im[..., None] * br
    bu_re = jnp.einsum('blgh,gph->blgp', uf, bb_re)
    bu_im = jnp.einsum('blgh,gph->blgp', uf, bb_im)
    a_re = jnp.broadcast_to(ab_re, bu_re.shape)
    a_im = jnp.broadcast_to(ab_im, bu_im.shape)
    _, _, x_re, x_im = lax.associative_scan(_complex_affine_combine, (a_re, a_im, bu_re, bu_im), axis=1)
    y = (jnp.einsum('blgp,ghp->blgh', x_re, s5_c_re.astype(f32))
         - jnp.einsum('blgp,ghp->blgh', x_im, s5_c_im.astype(f32))
         + s5_d.astype(f32) * uf)
    y = y.reshape(bsz, l, C_WIDTH)
    z = jax.nn.gelu(y)
    out = z * jax.nn.sigmoid(z @ w_glu.astype(f32) + b_glu.astype(f32))
    return out.astype(u.dtype)


def hybrid_mixer(h, w_in, conv_w, a_log, dt_bias, a_norm, sinks, b_norm,
                 lam_re, lam_im, log_dt, s5_b_re, s5_b_im, s5_c_re, s5_c_im, s5_d,
                 w_glu, b_glu, c_norm, w_out):
    proj = h @ w_in
    a_qkv, a_gate, a_alpha, a_beta, b_q, b_k, b_v, c_u = jnp.split(proj, _split_points(), axis=-1)
    ya = mixer_deltanet(a_qkv, a_gate, a_alpha, a_beta, conv_w, a_log, dt_bias, a_norm)
    yb = rms_norm(mixer_window_attention(b_q, b_k, b_v, sinks), b_norm)
    yc = rms_norm(mixer_s5(c_u, lam_re, lam_im, log_dt, s5_b_re, s5_b_im, s5_c_re, s5_c_im,
                           s5_d, w_glu, b_glu), c_norm)
    return jnp.concatenate([ya, yb, yc], axis=-1) @ w_out


def swiglu(x, wg, wu, wd):
    return (jax.nn.silu(x @ wg) * (x @ wu)) @ wd


def moe_swiglu(x, w_router, wg, wu, wd):
    b, l, d = x.shape
    xt = x.reshape(b * l, d)
    logits = (xt @ w_router).astype(jnp.float32)
    top_vals, top_idx = lax.top_k(logits, TOP_K)
    gates = jax.nn.softmax(top_vals, axis=-1)
    combine = jnp.sum(jax.nn.one_hot(top_idx, N_EXPERTS, dtype=jnp.float32) * gates[..., None], axis=1)
    y = jnp.zeros_like(xt)
    for e in range(N_EXPERTS):
        y = y + combine[:, e:e + 1].astype(xt.dtype) * swiglu(xt, wg[e], wu[e], wd[e])
    return y.reshape(b, l, d)


def setup_inputs(seed: int = 0) -> dict:
    key = jax.random.key(seed)
    ks = iter(jax.random.split(key, 40))
    nrm = lambda shape, scale: jax.random.normal(next(ks), shape, jnp.float32) * scale
    n_dense = (DEPTH + 1) // 2
    n_moe = DEPTH // 2
    dt_a = jnp.exp(jax.random.uniform(next(ks), (DEPTH, A_HEADS), jnp.float32, math.log(1e-3), math.log(1e-1)))
    lam_im_base = math.pi * jnp.arange(S5_STATE, dtype=jnp.float32)
    return {
        "x": nrm((BATCH, SEQ, D_MODEL), 1.0),
        "ln_gains": 1.0 + nrm((DEPTH, 4, D_MODEL), 0.01),
        "w_in": nrm((DEPTH, D_MODEL, D_IN), D_MODEL ** -0.5),
        "conv_w": nrm((DEPTH, CONV_WIDTH, 3 * A_WIDTH), CONV_WIDTH ** -0.5),
        "a_log": jnp.log(jax.random.uniform(next(ks), (DEPTH, A_HEADS), jnp.float32, 1.0, 16.0)),
        "dt_bias": dt_a + jnp.log(-jnp.expm1(-dt_a)),
        "a_norm": 1.0 + nrm((DEPTH, A_HEAD_DIM), 0.01),
        "sinks": nrm((DEPTH, B_HEADS), 1.0),
        "b_norm": 1.0 + nrm((DEPTH, B_WIDTH), 0.01),
        "lam_re": -0.5 + nrm((DEPTH, S5_GROUPS, S5_STATE), 0.01),
        "lam_im": lam_im_base + nrm((DEPTH, S5_GROUPS, S5_STATE), 0.01),
        "log_dt": jax.random.uniform(next(ks), (DEPTH, S5_GROUPS), jnp.float32, math.log(1e-3), math.log(1e-1)),
        "s5_b_re": nrm((DEPTH, S5_GROUPS, S5_STATE, S5_GROUP), (2 * S5_GROUP) ** -0.5),
        "s5_b_im": nrm((DEPTH, S5_GROUPS, S5_STATE, S5_GROUP), (2 * S5_GROUP) ** -0.5),
        "s5_c_re": nrm((DEPTH, S5_GROUPS, S5_GROUP, S5_STATE), (2 * S5_STATE) ** -0.5),
        "s5_c_im": nrm((DEPTH, S5_GROUPS, S5_GROUP, S5_STATE), (2 * S5_STATE) ** -0.5),
        "s5_d": nrm((DEPTH, S5_GROUPS, S5_GROUP), 1.0),
        "w_glu": nrm((DEPTH, C_WIDTH, C_WIDTH), C_WIDTH ** -0.5),
        "b_glu": nrm((DEPTH, C_WIDTH), 0.01),
        "c_norm": 1.0 + nrm((DEPTH, C_WIDTH), 0.01),
        "w_out": nrm((DEPTH, D_MIX, D_MODEL), D_MIX ** -0.5),
        "ffn_wg": nrm((n_dense, D_MODEL, D_FF), D_MODEL ** -0.5),
        "ffn_wu": nrm((n_dense, D_MODEL, D_FF), D_MODEL ** -0.5),
        "ffn_wd": nrm((n_dense, D_FF, D_MODEL), D_FF ** -0.5),
        "w_router": nrm((n_moe, D_MODEL, N_EXPERTS), D_MODEL ** -0.5),
        "moe_wg": nrm((n_moe, N_EXPERTS, D_MODEL, D_FF), D_MODEL ** -0.5),
        "moe_wu": nrm((n_moe, N_EXPERTS, D_MODEL, D_FF), D_MODEL ** -0.5),
        "moe_wd": nrm((n_moe, N_EXPERTS, D_FF, D_MODEL), D_FF ** -0.5),
    }


def reference(x, ln_gains, w_in, conv_w, a_log, dt_bias, a_norm, sinks, b_norm,
              lam_re, lam_im, log_dt, s5_b_re, s5_b_im, s5_c_re, s5_c_im, s5_d,
              w_glu, b_glu, c_norm, w_out, ffn_wg, ffn_wu, ffn_wd,
              w_router, moe_wg, moe_wu, moe_wd):
    for i in range(DEPTH):
        g = ln_gains[i]
        mix = hybrid_mixer(rms_norm(x, g[0]), w_in[i], conv_w[i], a_log[i], dt_bias[i], a_norm[i],
                           sinks[i], b_norm[i], lam_re[i], lam_im[i], log_dt[i], s5_b_re[i],
                           s5_b_im[i], s5_c_re[i], s5_c_im[i], s5_d[i], w_glu[i], b_glu[i],
                           c_norm[i], w_out[i])
        x = x + rms_norm(mix, g[1])
        hn = rms_norm(x, g[2])
        j = i // 2
        if i % 2 == 0:
            f = swiglu(hn, ffn_wg[j], ffn_wu[j], ffn_wd[j])
        else:
            f = moe_swiglu(hn, w_router[j], moe_wg[j], moe_wu[j], moe_wd[j])
        x = x + rms_norm(f, g[3])
    return x
```

```python
import functools
import math

import jax
import jax.numpy as jnp
from jax import lax
from jax.experimental import pallas as pl
from jax.experimental.pallas import tpu as pltpu

F32 = jnp.float32
BF16 = jnp.bfloat16
HI = lax.Precision.HIGHEST

D_MODEL = 2048
CHUNK = 64
A_HEADS = 8
A_HEAD_DIM = 128
A_WIDTH = A_HEADS * A_HEAD_DIM
B_HEADS = 8
B_KV_HEADS = 2
B_HEAD_DIM = 64
B_WIDTH = B_HEADS * B_HEAD_DIM
B_KV_DIM = B_KV_HEADS * B_HEAD_DIM
WINDOW_CHUNKS = 2
C_WIDTH = 512
S5_GROUP = 16
S5_GROUPS = 32
S5_STATE = 64
D_FF = 7168
N_EXPERTS = 8
EPS = 1e-6

LANES = 128
SUBLANES = 8
VMEM_LIMIT = 56 * 1024 * 1024

COL_QKV = 0
COL_GATE = 3 * A_WIDTH
COL_BQ = COL_GATE + A_WIDTH
COL_BK = COL_BQ + B_WIDTH
COL_BV = COL_BK + B_KV_DIM
COL_CU = COL_BV + B_KV_DIM
N_MAIN = COL_CU + C_WIDTH
IN_TN = N_MAIN // 3

S5_T = 64
S5_W = S5_T * S5_GROUP

MOE_TM = 512
FFN_TF = 512


def _params(*sem):
    return pltpu.CompilerParams(dimension_semantics=sem, vmem_limit_bytes=VMEM_LIMIT)


def _rms(x, gain):
    return x * lax.rsqrt(jnp.mean(x * x, axis=-1, keepdims=True) + EPS) * gain


def _sigmoid(x):
    return 1.0 / (1.0 + jnp.exp(-x))


def _silu(x):
    return x * _sigmoid(x)


def _inproj_kernel(x_ref, g_ref, w_ref, wab_ref, o_ref, ab_ref, xn_ref):
    @pl.when(pl.program_id(1) == 0)
    def _():
        xn_ref[...] = _rms(x_ref[...], g_ref[...]).astype(BF16)
        ab_ref[...] = jnp.dot(xn_ref[...], wab_ref[...], preferred_element_type=F32)

    o_ref[...] = jnp.dot(xn_ref[...], w_ref[...], preferred_element_type=F32).astype(BF16)


def _inproj(x, gain, w_main, w_ab, tm=512):
    n = x.shape[0]
    return pl.pallas_call(
        _inproj_kernel,
        grid=(n // tm, N_MAIN // IN_TN),
        in_specs=[
            pl.BlockSpec((tm, D_MODEL), lambda i, j: (i, 0)),
            pl.BlockSpec((1, D_MODEL), lambda i, j: (0, 0)),
            pl.BlockSpec((D_MODEL, IN_TN), lambda i, j: (0, j)),
            pl.BlockSpec((D_MODEL, LANES), lambda i, j: (0, 0)),
        ],
        out_specs=[
            pl.BlockSpec((tm, IN_TN), lambda i, j: (i, j)),
            pl.BlockSpec((tm, LANES), lambda i, j: (i, 0)),
        ],
        out_shape=[
            jax.ShapeDtypeStruct((n, N_MAIN), BF16),
            jax.ShapeDtypeStruct((n, LANES), F32),
        ],
        scratch_shapes=[pltpu.VMEM((tm, D_MODEL), BF16)],
        compiler_params=_params("parallel", "arbitrary"),
        name="inproj",
    )(x, gain, w_main, w_ab)


def _pick_col(x, idx):
    lane = lax.broadcasted_iota(jnp.int32, x.shape, 1)
    return jnp.sum(jnp.where(lane == idx, x, 0.0), axis=1, keepdims=True)


def _dot_hi(a, b):
    return jnp.dot(a, b, preferred_element_type=F32, precision=HI)


def _delta_kernel(q_ref, k_ref, v_ref, gt_ref, ab_ref, cwq_ref, cwk_ref, cwv_ref,
                  alog_ref, dtb_ref, anorm_ref, o_ref, s_ref):
    h = pl.program_id(1)
    seq = q_ref.shape[0]
    c = CHUNK
    row = lax.broadcasted_iota(jnp.int32, (c, c), 0)
    col = lax.broadcasted_iota(jnp.int32, (c, c), 1)
    incl = row >= col
    strict = row > col
    same_blk = (row >> 4) == (col >> 4)
    eye = jnp.where(row == col, 1.0, 0.0).astype(F32)
    tril = jnp.where(incl, 1.0, 0.0).astype(F32)
    sel_h = jnp.where(lax.broadcasted_iota(jnp.int32, (SUBLANES, LANES), 1) == h, 1.0, 0.0).astype(F32)
    s_ref[...] = jnp.zeros_like(s_ref)

    def conv_silu(ref, cw_ref, ci, start, hist_start):
        cur = ref[pl.ds(start, c), :].astype(F32)
        hist = ref[pl.ds(hist_start, 16), :].astype(F32)
        hist = jnp.where(ci > 0, hist, 0.0)
        ext = jnp.concatenate([hist, cur], axis=0)
        w = cw_ref[...]
        y = ext[13:13 + c] * w[0:1]
        for j in range(1, 4):
            y = y + ext[13 + j:13 + j + c] * w[j:j + 1]
        return _silu(y)

    def l2n(x):
        return x * lax.rsqrt(jnp.sum(x * x, axis=-1, keepdims=True) + EPS)

    def body(ci, carry):
        start = pl.multiple_of(ci * c, c)
        hist_start = pl.multiple_of(jnp.maximum(start - 16, 0), 16)
        q = l2n(conv_silu(q_ref, cwq_ref, ci, start, hist_start)) * (A_HEAD_DIM ** -0.5)
        k = l2n(conv_silu(k_ref, cwk_ref, ci, start, hist_start))
        v = conv_silu(v_ref, cwv_ref, ci, start, hist_start)

        ab = ab_ref[pl.ds(start, c), :]
        z = ab + dtb_ref[...]
        softplus = jnp.maximum(z, 0.0) + jnp.log1p(jnp.exp(-jnp.abs(z)))
        g_all = -jnp.exp(alog_ref[...]) * softplus
        gc_all = _dot_hi(tril, g_all)
        gc = _pick_col(gc_all, h)
        beta = _pick_col(_sigmoid(ab), A_HEADS + h)
        gc_row = lax.dot_general(sel_h, gc_all, (((1,), (1,)), ((), ())),
                                 preferred_element_type=F32, precision=HI)[0:1, :]
        gdiff = gc - gc_row
        dec_incl = jnp.exp(jnp.where(incl, gdiff, -jnp.inf))
        dec_strict = jnp.where(strict, dec_incl, 0.0)
        egc = jnp.exp(gc)
        g_last = gc[c - 1:c, :]

        kb = k * beta
        low = lax.dot_general(kb, k, (((1,), (1,)), ((), ())),
                              preferred_element_type=F32, precision=HI) * dec_strict
        nd = -jnp.where(same_blk, low, 0.0)
        off = jnp.where(same_blk, 0.0, low)
        p = eye + nd
        m = _dot_hi(nd, nd)
        p = p + _dot_hi(p, m)
        m = _dot_hi(m, m)
        p = p + _dot_hi(p, m)
        m = _dot_hi(m, m)
        p = p + _dot_hi(p, m)
        qm = _dot_hi(p, off)
        qm2 = _dot_hi(qm, qm)
        tinv = _dot_hi(eye - qm + qm2 - _dot_hi(qm, qm2), p)
        u = _dot_hi(tinv, v * beta)
        w = _dot_hi(tinv, kb * egc)

        attn = lax.dot_general(q.astype(BF16), k.astype(BF16), (((1,), (1,)), ((), ())),
                               preferred_element_type=F32) * dec_incl
        q_dec = (q * egc).astype(BF16)
        k_dec = (k * jnp.exp(g_last - gc)).astype(BF16)

        state = s_ref[...]
        state_b = state.astype(BF16)
        v_new = u - jnp.dot(w.astype(BF16), state_b, preferred_element_type=F32)
        v_new_b = v_new.astype(BF16)
        o = (jnp.dot(q_dec, state_b, preferred_element_type=F32)
             + jnp.dot(attn.astype(BF16), v_new_b, preferred_element_type=F32))
        s_ref[...] = state * jnp.exp(g_last) + lax.dot_general(
            k_dec, v_new_b, (((0,), (0,)), ((), ())), preferred_element_type=F32)

        gate = gt_ref[pl.ds(start, c), :].astype(F32)
        o_ref[pl.ds(start, c), :] = (_rms(o, anorm_ref[...]) * _silu(gate)).astype(o_ref.dtype)
        return carry

    lax.fori_loop(0, seq // c, body, 0)


def _deltanet(proj3, ab3, conv_w, alog_row, dtb_row, a_norm):
    bsz, seq, _ = proj3.shape
    hblk = lambda base: pl.BlockSpec((None, seq, LANES), lambda b, h: (b, 0, base + h))
    cblk = lambda base: pl.BlockSpec((4, LANES), lambda b, h: (0, base + h))
    row = pl.BlockSpec((1, LANES), lambda b, h: (0, 0))
    return pl.pallas_call(
        _delta_kernel,
        grid=(bsz, A_HEADS),
        in_specs=[hblk(0), hblk(A_HEADS), hblk(2 * A_HEADS), hblk(3 * A_HEADS),
                  pl.BlockSpec((None, seq, LANES), lambda b, h: (b, 0, 0)),
                  cblk(0), cblk(A_HEADS), cblk(2 * A_HEADS), row, row, row],
        out_specs=pl.BlockSpec((None, seq, LANES), lambda b, h: (b, 0, h)),
        out_shape=jax.ShapeDtypeStruct((bsz, seq, A_WIDTH), BF16),
        scratch_shapes=[pltpu.VMEM((A_HEAD_DIM, A_HEAD_DIM), F32)],
        compiler_params=_params("parallel", "parallel"),
        name="deltanet",
    )(proj3, proj3, proj3, proj3, ab3, conv_w, conv_w, conv_w, alog_row, dtb_row, a_norm)


def _swa_kernel(q_ref, k_ref, v_ref, sink_ref, bn_ref, o_ref, acc_ref):
    tq = q_ref.shape[0]
    band = WINDOW_CHUNKS * CHUNK
    kw = tq + band
    start = pl.program_id(1) * tq
    off = pl.multiple_of(jnp.maximum(start - band, 0), band)
    kk = k_ref[pl.ds(off, kw), :]
    vv = v_ref[pl.ds(off, kw), :]
    qchunk = (start + lax.broadcasted_iota(jnp.int32, (tq, kw), 0)) >> 6
    kchunk = (off + lax.broadcasted_iota(jnp.int32, (tq, kw), 1)) >> 6
    valid = (kchunk <= qchunk) & (kchunk >= qchunk - WINDOW_CHUNKS)
    grp = B_HEADS // B_KV_HEADS
    for hd in range(B_HEADS):
        kv = hd // grp
        qh = q_ref[:, hd * B_HEAD_DIM:(hd + 1) * B_HEAD_DIM]
        kh = kk[:, kv * B_HEAD_DIM:(kv + 1) * B_HEAD_DIM]
        vh = vv[:, kv * B_HEAD_DIM:(kv + 1) * B_HEAD_DIM]
        s = lax.dot_general(qh, kh, (((1,), (1,)), ((), ())), preferred_element_type=F32) * (B_HEAD_DIM ** -0.5)
        s = jnp.where(valid, s, -jnp.inf)
        sink = sink_ref[0:1, hd:hd + 1]
        m = jnp.maximum(jnp.max(s, axis=-1, keepdims=True), sink)
        p = jnp.exp(s - m)
        denom = jnp.sum(p, axis=-1, keepdims=True) + jnp.exp(sink - m)
        pn = (p / denom).astype(BF16)
        acc_ref[:, hd * B_HEAD_DIM:(hd + 1) * B_HEAD_DIM] = jnp.dot(pn, vh, preferred_element_type=F32)
    o_ref[...] = _rms(acc_ref[...], bn_ref[...]).astype(o_ref.dtype)


def _swa(proj3, sinks_row, b_norm, tq=128):
    bsz, seq, _ = proj3.shape
    return pl.pallas_call(
        _swa_kernel,
        grid=(bsz, seq // tq),
        in_specs=[
            pl.BlockSpec((None, tq, B_WIDTH), lambda b, i: (b, i, COL_BQ // B_WIDTH)),
            pl.BlockSpec((None, seq, B_KV_DIM), lambda b, i: (b, 0, COL_BK // B_KV_DIM)),
            pl.BlockSpec((None, seq, B_KV_DIM), lambda b, i: (b, 0, COL_BV // B_KV_DIM)),
            pl.BlockSpec((1, LANES), lambda b, i: (0, 0)),
            pl.BlockSpec((1, B_WIDTH), lambda b, i: (0, 0)),
        ],
        out_specs=pl.BlockSpec((None, tq, B_WIDTH), lambda b, i: (b, i, 0)),
        out_shape=jax.ShapeDtypeStruct((bsz, seq, B_WIDTH), BF16),
        scratch_shapes=[pltpu.VMEM((tq, B_WIDTH), F32)],
        compiler_params=_params("parallel", "arbitrary"),
        name="swa",
    )(proj3, proj3, proj3, sinks_row, b_norm)


def _cmul(ar, ai, br, bi):
    return ar * br - ai * bi, ar * bi + ai * br


def _s5_kernel(lre_ref, lim_ref, ldt_ref, bre_ref, bim_ref, bxre_ref, bxim_ref, cxre_ref, cxim_ref,
               d_ref, u_ref, y_ref, m_ref, *, chunks_per_seq):
    t, w = S5_T, S5_W
    nct = u_ref.shape[0]
    dt = jnp.exp(ldt_ref[...])
    lr = jnp.minimum(lre_ref[...], -1e-4)
    li = lim_ref[...]
    mag = jnp.exp(lr * dt)
    a_re = mag * jnp.cos(li * dt)
    a_im = mag * jnp.sin(li * dt)
    den = lr * lr + li * li
    nr = a_re - 1.0
    coef_re = (nr * lr + a_im * li) / den
    coef_im = (a_im * lr - nr * li) / den

    n_sq = (t.bit_length() - 1) + max(chunks_per_seq.bit_length() - 1, 0)
    sq = [(a_re, a_im)]
    for _ in range(n_sq):
        sq.append(_cmul(*sq[-1], *sq[-1]))

    lane = lax.broadcasted_iota(jnp.int32, (S5_STATE, w), 1)
    frame = lane >> 4

    def powers(idx):
        pr = jnp.ones((S5_STATE, w), F32)
        pi = jnp.zeros((S5_STATE, w), F32)
        for b in range(t.bit_length() - 1):
            bit = ((idx >> b) & 1) == 1
            fr = jnp.where(bit, sq[b][0], 1.0)
            fi = jnp.where(bit, sq[b][1], 0.0)
            pr, pi = _cmul(pr, pi, fr, fi)
        return pr, pi

    pw_re, pw_im = powers(frame)
    e_re, e_im = _cmul(pw_re, pw_im, cxre_ref[...], cxim_ref[...])
    bb_re, bb_im = _cmul(coef_re, coef_im, bre_ref[...], bim_ref[...])
    tn = (((0,), (0,)), ((), ()))
    krow = (lax.dot_general(bb_re, e_re, tn, preferred_element_type=F32, precision=HI)
            - lax.dot_general(bb_im, e_im, tn, preferred_element_type=F32, precision=HI))
    r16 = lax.broadcasted_iota(jnp.int32, (S5_GROUP, w), 0)
    l16 = lax.broadcasted_iota(jnp.int32, (S5_GROUP, w), 1)
    krow = krow + jnp.where(l16 == r16, d_ref[...], 0.0)
    for s in range(t):
        shifted = krow if s == 0 else jnp.where(l16 >= S5_GROUP * s, pltpu.roll(krow, S5_GROUP * s, 1), 0.0)
        m_ref[S5_GROUP * s:S5_GROUP * (s + 1), :] = shifted.astype(BF16)

    u = u_ref[...]
    y = jnp.dot(u, m_ref[...], preferred_element_type=F32)

    rv_re, rv_im = powers(t - 1 - frame)
    bx_re, bx_im = _cmul(coef_re, coef_im, bxre_ref[...], bxim_ref[...])
    st_re, st_im = _cmul(rv_re, rv_im, bx_re, bx_im)
    bst = jnp.concatenate([st_re, st_im], axis=0).astype(BF16)
    xloc = lax.dot_general(bst, u, (((1,), (1,)), ((), ())), preferred_element_type=F32)
    xr, xi = xloc[:S5_STATE], xloc[S5_STATE:]
    cpos = lax.broadcasted_iota(jnp.int32, (S5_STATE, nct), 1) & (chunks_per_seq - 1)
    tb = t.bit_length() - 1
    for kb in range(chunks_per_seq.bit_length() - 1):
        sh = 1 << kb
        keep = cpos >= sh
        rr = jnp.where(keep, pltpu.roll(xr, sh, 1), 0.0)
        ri = jnp.where(keep, pltpu.roll(xi, sh, 1), 0.0)
        dr, di = _cmul(sq[tb + kb][0], sq[tb + kb][1], rr, ri)
        xr, xi = xr + dr, xi + di
    keep = cpos >= 1
    xp = jnp.concatenate([jnp.where(keep, pltpu.roll(xr, 1, 1), 0.0),
                          jnp.where(keep, pltpu.roll(xi, 1, 1), 0.0)], axis=0).astype(BF16)
    e1_re, e1_im = _cmul(e_re, e_im, a_re, a_im)
    e1 = jnp.concatenate([e1_re, -e1_im], axis=0).astype(BF16)
    y = y + lax.dot_general(xp, e1, tn, preferred_element_type=F32)
    y_ref[...] = y.astype(y_ref.dtype)


def _s5(proj3, prm):
    bsz, seq, _ = proj3.shape
    cps = seq // S5_T
    nct = bsz * cps
    assert cps & (cps - 1) == 0
    u = proj3[:, :, COL_CU:].reshape(bsz, cps, S5_T, S5_GROUPS, S5_GROUP)
    u = jnp.transpose(u, (3, 0, 1, 2, 4)).reshape(S5_GROUPS, nct, S5_W)
    col = pl.BlockSpec((None, S5_STATE, 1), lambda g: (g, 0, 0))
    nat = pl.BlockSpec((None, S5_STATE, S5_GROUP), lambda g: (g, 0, 0))
    wide = pl.BlockSpec((None, S5_STATE, S5_W), lambda g: (g, 0, 0))
    y = pl.pallas_call(
        functools.partial(_s5_kernel, chunks_per_seq=cps),
        grid=(S5_GROUPS,),
        in_specs=[col, col, pl.BlockSpec((None, 1, 1), lambda g: (g, 0, 0)), nat, nat, wide, wide, wide, wide,
                  pl.BlockSpec((None, S5_GROUP, 1), lambda g: (g, 0, 0)),
                  pl.BlockSpec((None, nct, S5_W), lambda g: (g, 0, 0))],
        out_specs=pl.BlockSpec((None, nct, S5_W), lambda g: (g, 0, 0)),
        out_shape=jax.ShapeDtypeStruct((S5_GROUPS, nct, S5_W), BF16),
        scratch_shapes=[pltpu.VMEM((S5_W, S5_W), BF16)],
        compiler_params=_params("parallel"),
        name="s5",
    )(prm["lam_re"], prm["lam_im"], prm["log_dt"], prm["b_re"], prm["b_im"], prm["bx_re"], prm["bx_im"],
      prm["cx_re"], prm["cx_im"], prm["s5_d"], u)
    y = jnp.transpose(y.reshape(S5_GROUPS, nct, S5_T, S5_GROUP), (1, 2, 0, 3))
    return y.reshape(bsz, seq, C_WIDTH)


def _gelu_tanh(x):
    return 0.5 * x * (1.0 + jnp.tanh(math.sqrt(2.0 / math.pi) * (x + 0.044715 * (x * x * x))))


def _post_kernel(*refs, with_router):
    if with_router:
        (ya_ref, yb_ref, yc_ref, wglu_ref, bglu_ref, cn_ref, wout_ref, x_ref, g1_ref, g2_ref, wrt_ref,
         xo_ref, hn_ref, lt_ref) = refs
    else:
        (ya_ref, yb_ref, yc_ref, wglu_ref, bglu_ref, cn_ref, wout_ref, x_ref, g1_ref, g2_ref,
         xo_ref, hn_ref) = refs
    z = _gelu_tanh(yc_ref[...].astype(F32))
    gl = jnp.dot(z.astype(BF16), wglu_ref[...], preferred_element_type=F32) + bglu_ref[...]
    yc = _rms(z * _sigmoid(gl), cn_ref[...]).astype(BF16)
    cat = jnp.concatenate([ya_ref[...], yb_ref[...], yc], axis=1)
    mix = jnp.dot(cat, wout_ref[...], preferred_element_type=F32)
    xo = x_ref[...] + _rms(mix, g1_ref[...])
    xo_ref[...] = xo
    hn = _rms(xo, g2_ref[...])
    hn_ref[...] = hn.astype(hn_ref.dtype)
    if with_router:
        lt_ref[...] = lax.dot_general(wrt_ref[...], hn, (((1,), (1,)), ((), ())),
                                      preferred_element_type=F32, precision=HI)


def _post(ya, yb, yc, w_glu, b_glu, c_norm, w_out, x, g1, g2, w_rt=None, tm=256):
    n = x.shape[0]
    with_router = w_rt is not None
    rows = lambda width: pl.BlockSpec((tm, width), lambda i: (i, 0))
    const = lambda shape: pl.BlockSpec(shape, lambda i: (0, 0))
    in_specs = [rows(A_WIDTH), rows(B_WIDTH), rows(C_WIDTH), const((C_WIDTH, C_WIDTH)), const((1, C_WIDTH)),
                const((1, C_WIDTH)), const((D_MODEL, D_MODEL)), rows(D_MODEL), const((1, D_MODEL)),
                const((1, D_MODEL))]
    out_specs = [rows(D_MODEL), rows(D_MODEL)]
    out_shape = [jax.ShapeDtypeStruct((n, D_MODEL), F32),
                 jax.ShapeDtypeStruct((n, D_MODEL), F32 if with_router else BF16)]
    args = [ya, yb, yc, w_glu, b_glu, c_norm, w_out, x, g1, g2]
    if with_router:
        in_specs.append(const((N_EXPERTS, D_MODEL)))
        out_specs.append(pl.BlockSpec((N_EXPERTS, tm), lambda i: (0, i)))
        out_shape.append(jax.ShapeDtypeStruct((N_EXPERTS, n), F32))
        args.append(w_rt)
    return pl.pallas_call(
        functools.partial(_post_kernel, with_router=with_router),
        grid=(n // tm,),
        in_specs=in_specs,
        out_specs=out_specs,
        out_shape=out_shape,
        compiler_params=_params("parallel"),
        name="post_router" if with_router else "post",
    )(*args)


def _swiglu_partial(h, wg_ref, wu_ref, wd_ref):
    a = jnp.dot(h, wg_ref[...], preferred_element_type=F32)
    b = jnp.dot(h, wu_ref[...], preferred_element_type=F32)
    return jnp.dot((_silu(a) * b).astype(BF16), wd_ref[...], preferred_element_type=F32)


def _ffn_kernel(h_ref, wg_ref, wu_ref, wd_ref, x_ref, g_ref, o_ref, acc_ref):
    f = pl.program_id(1)
    part = _swiglu_partial(h_ref[...], wg_ref, wu_ref, wd_ref)

    @pl.when(f == 0)
    def _():
        acc_ref[...] = part

    @pl.when(f > 0)
    def _():
        acc_ref[...] += part

    @pl.when(f == pl.num_programs(1) - 1)
    def _():
        o_ref[...] = x_ref[...] + _rms(acc_ref[...], g_ref[...])


def _ffn(hn, wg, wu, wd, x, g3, tm=512, tf=FFN_TF):
    n = x.shape[0]
    return pl.pallas_call(
        _ffn_kernel,
        grid=(n // tm, D_FF // tf),
        in_specs=[
            pl.BlockSpec((tm, D_MODEL), lambda i, f: (i, 0)),
            pl.BlockSpec((D_MODEL, tf), lambda i, f: (0, f)),
            pl.BlockSpec((D_MODEL, tf), lambda i, f: (0, f)),
            pl.BlockSpec((tf, D_MODEL), lambda i, f: (f, 0)),
            pl.BlockSpec((tm, D_MODEL), lambda i, f: (i, 0)),
            pl.BlockSpec((1, D_MODEL), lambda i, f: (0, 0)),
        ],
        out_specs=pl.BlockSpec((tm, D_MODEL), lambda i, f: (i, 0)),
        out_shape=jax.ShapeDtypeStruct((n, D_MODEL), F32),
        scratch_shapes=[pltpu.VMEM((tm, D_MODEL), F32)],
        compiler_params=_params("parallel", "arbitrary"),
        name="ffn_dense",
    )(hn, wg, wu, wd, x, g3)


def _router_kernel(lt_ref, slot_ref, gate_ref, meta_ref, *, tile_rows):
    lt = lt_ref[...]
    n = lt.shape[1]
    e = lax.broadcasted_iota(jnp.int32, lt.shape, 0)
    m1 = jnp.max(lt, axis=0, keepdims=True)
    sel1 = e == jnp.min(jnp.where(lt == m1, e, N_EXPERTS), axis=0, keepdims=True)
    lt2 = jnp.where(sel1, -jnp.inf, lt)
    m2 = jnp.max(lt2, axis=0, keepdims=True)
    sel2 = e == jnp.min(jnp.where(lt2 == m2, e, N_EXPERTS), axis=0, keepdims=True)
    ex = jnp.exp(m2 - m1)
    gate_ref[...] = jnp.concatenate([1.0 / (1.0 + ex), ex / (1.0 + ex)], axis=0)

    member = jnp.where(sel1 | sel2, 1.0, 0.0).astype(F32)
    ch = min(n, 512)
    tri = jnp.where(lax.broadcasted_iota(jnp.int32, (ch, ch), 0) <= lax.broadcasted_iota(jnp.int32, (ch, ch), 1),
                    1.0, 0.0).astype(BF16)
    carry = jnp.zeros((N_EXPERTS, 1), F32)
    pos = []
    for j in range(n // ch):
        mj = member[:, j * ch:(j + 1) * ch]
        inc = jnp.dot(mj.astype(BF16), tri, preferred_element_type=F32) + carry
        pos.append(inc - mj)
        carry = inc[:, ch - 1:ch]
    pos = jnp.concatenate(pos, axis=1)
    tiles = jnp.floor((carry + (tile_rows - 1)) * (1.0 / tile_rows))
    lower = jnp.where(lax.broadcasted_iota(jnp.int32, (N_EXPERTS, N_EXPERTS), 1)
                      < lax.broadcasted_iota(jnp.int32, (N_EXPERTS, N_EXPERTS), 0), 1.0, 0.0).astype(F32)
    tiles_b = jnp.broadcast_to(tiles, (N_EXPERTS, LANES))
    first_tile = _dot_hi(lower, tiles_b)
    base = first_tile[:, 0:1] * tile_rows
    slot1 = jnp.sum(jnp.where(sel1, base + pos, 0.0), axis=0, keepdims=True)
    slot2 = jnp.sum(jnp.where(sel2, base + pos, 0.0), axis=0, keepdims=True)
    slot_ref[...] = jnp.concatenate([slot1, slot2], axis=0).astype(jnp.int32)

    ends = first_tile + tiles_b
    total = ends[N_EXPERTS - 1:N_EXPERTS, :]
    tidx = jnp.minimum(lax.broadcasted_iota(jnp.int32, (N_EXPERTS, LANES), 1).astype(F32), total - 1.0)
    tile_expert = jnp.sum(jnp.where(ends <= tidx, 1.0, 0.0), axis=0, keepdims=True)
    meta = jnp.concatenate([tile_expert, total, jnp.zeros((SUBLANES - 2, LANES), F32)], axis=0)
    meta_ref[...] = meta.astype(jnp.int32)


def _router(logits_t, tile_rows):
    n = logits_t.shape[1]
    return pl.pallas_call(
        functools.partial(_router_kernel, tile_rows=tile_rows),
        out_shape=[jax.ShapeDtypeStruct((2, n), jnp.int32), jax.ShapeDtypeStruct((2, n), F32),
                   jax.ShapeDtypeStruct((SUBLANES, LANES), jnp.int32)],
        compiler_params=pltpu.CompilerParams(vmem_limit_bytes=VMEM_LIMIT),
        name="router",
    )(logits_t)


def _row_copy(src, src_row, dst, dst_row, sem):
    return pltpu.make_async_copy(src.at[pl.ds(src_row, 1)], dst.at[pl.ds(dst_row, 1)], sem)


def _dispatch_kernel(slot_ref, x_ref, xs_in_ref, xs_ref, sem):
    del xs_in_ref
    td = x_ref.shape[0]

    def issue(t, c):
        for k in range(2):
            _row_copy(x_ref, t, xs_ref, slot_ref[0, k * td + t], sem).start()
        return c

    def drain(t, c):
        for k in range(2):
            _row_copy(x_ref, 0, xs_ref, 0, sem).wait()
        return c

    lax.fori_loop(0, td, issue, 0)
    lax.fori_loop(0, td, drain, 0)


def _tile_slots(slots, rows):
    n = slots.shape[1]
    return jnp.transpose(slots.reshape(2, n // rows, rows), (1, 0, 2)).reshape(n // rows, 1, 2 * rows)


def _dispatch(hn, slots, n_slots, td=256):
    n = hn.shape[0]
    return pl.pallas_call(
        _dispatch_kernel,
        grid=(n // td,),
        in_specs=[
            pl.BlockSpec((None, 1, 2 * td), lambda i: (i, 0, 0), memory_space=pltpu.SMEM),
            pl.BlockSpec((td, D_MODEL), lambda i: (i, 0)),
            pl.BlockSpec(memory_space=pl.ANY),
        ],
        out_specs=pl.BlockSpec(memory_space=pl.ANY),
        out_shape=jax.ShapeDtypeStruct((n_slots, D_MODEL), F32),
        scratch_shapes=[pltpu.SemaphoreType.DMA(())],
        input_output_aliases={2: 0},
        compiler_params=_params("arbitrary"),
        name="moe_dispatch",
    )(_tile_slots(slots, td), hn, jnp.zeros((n_slots, D_MODEL), F32))


def _moe_kernel(te_ref, nu_ref, xs_ref, wg_ref, wu_ref, wd_ref, ys_ref, xb_ref, acc_ref):
    del te_ref
    i = pl.program_id(0)
    f = pl.program_id(1)

    @pl.when(i < nu_ref[0])
    def _():
        @pl.when(f == 0)
        def _():
            xb_ref[...] = xs_ref[...].astype(BF16)

        part = _swiglu_partial(xb_ref[...], wg_ref, wu_ref, wd_ref)

        @pl.when(f == 0)
        def _():
            acc_ref[...] = part

        @pl.when(f > 0)
        def _():
            acc_ref[...] += part

        @pl.when(f == pl.num_programs(1) - 1)
        def _():
            ys_ref[...] = acc_ref[...]

    @pl.when((i >= nu_ref[0]) & (f == pl.num_programs(1) - 1))
    def _():
        ys_ref[...] = jnp.zeros_like(ys_ref)


def _moe_ffn(xs, tile_expert, n_used, wg, wu, wd, tm=MOE_TM, tf=FFN_TF):
    n_slots = xs.shape[0]
    nf = D_FF // tf
    row_blk = lambda i, f, te, nu: (jnp.minimum(i, nu[0] - 1), 0)
    fcol = lambda i, f, nu: jnp.where(i < nu[0], f, nf - 1)
    grid_spec = pltpu.PrefetchScalarGridSpec(
        num_scalar_prefetch=2,
        grid=(n_slots // tm, nf),
        in_specs=[
            pl.BlockSpec((tm, D_MODEL), row_blk),
            pl.BlockSpec((None, D_MODEL, tf), lambda i, f, te, nu: (te[i], 0, fcol(i, f, nu))),
            pl.BlockSpec((None, D_MODEL, tf), lambda i, f, te, nu: (te[i], 0, fcol(i, f, nu))),
            pl.BlockSpec((None, tf, D_MODEL), lambda i, f, te, nu: (te[i], fcol(i, f, nu), 0)),
        ],
        out_specs=pl.BlockSpec((tm, D_MODEL), lambda i, f, te, nu: (i, 0)),
        scratch_shapes=[pltpu.VMEM((tm, D_MODEL), BF16), pltpu.VMEM((tm, D_MODEL), F32)],
    )
    return pl.pallas_call(
        _moe_kernel,
        grid_spec=grid_spec,
        out_shape=jax.ShapeDtypeStruct((n_slots, D_MODEL), F32),
        compiler_params=_params("arbitrary", "arbitrary"),
        name="moe_ffn",
    )(tile_expert, n_used, xs, wg, wu, wd)


def _combine_kernel(slot_ref, gate_ref, x_ref, g_ref, ys_ref, o_ref, buf_ref, sem):
    tc = x_ref.shape[0]

    def issue(t, c):
        for k in range(2):
            _row_copy(ys_ref, slot_ref[0, k * tc + t], buf_ref.at[k], t, sem).start()
        return c

    def drain(t, c):
        for k in range(2):
            _row_copy(ys_ref, 0, buf_ref.at[k], 0, sem).wait()
        return c

    lax.fori_loop(0, tc, issue, 0)
    lax.fori_loop(0, tc, drain, 0)
    gts = gate_ref[...]
    y = buf_ref[0] * gts[:, 0:1] + buf_ref[1] * gts[:, 1:2]
    o_ref[...] = x_ref[...] + _rms(y, g_ref[...])


def _combine(ys, slots, gates, x, g3, tc=256):
    n = x.shape[0]
    return pl.pallas_call(
        _combine_kernel,
        grid=(n // tc,),
        in_specs=[
            pl.BlockSpec((None, 1, 2 * tc), lambda i: (i, 0, 0), memory_space=pltpu.SMEM),
            pl.BlockSpec((tc, 2), lambda i: (i, 0)),
            pl.BlockSpec((tc, D_MODEL), lambda i: (i, 0)),
            pl.BlockSpec((1, D_MODEL), lambda i: (0, 0)),
            pl.BlockSpec(memory_space=pl.ANY),
        ],
        out_specs=pl.BlockSpec((tc, D_MODEL), lambda i: (i, 0)),
        out_shape=jax.ShapeDtypeStruct((n, D_MODEL), F32),
        scratch_shapes=[pltpu.VMEM((2, tc, D_MODEL), F32), pltpu.SemaphoreType.DMA(())],
        compiler_params=_params("arbitrary"),
        name="moe_combine",
    )(_tile_slots(slots, tc), jnp.transpose(gates), x, g3, ys)


def _prep_layer(p, i):
    w_in = p["w_in"][i]
    alpha_beta = w_in[:, 4 * A_WIDTH:4 * A_WIDTH + 2 * A_HEADS]
    out = {
        "w_main": jnp.concatenate([w_in[:, :4 * A_WIDTH], w_in[:, 4 * A_WIDTH + 2 * A_HEADS:]], axis=1).astype(BF16),
        "w_ab": jnp.pad(alpha_beta, ((0, 0), (0, LANES - 2 * A_HEADS))).astype(BF16),
        "alog_row": jnp.pad(p["a_log"][i], (0, LANES - A_HEADS))[None],
        "dtb_row": jnp.pad(p["dt_bias"][i], (0, LANES - A_HEADS))[None],
        "sinks_row": jnp.pad(p["sinks"][i], (0, LANES - B_HEADS))[None],
        "lam_re": p["lam_re"][i][:, :, None],
        "lam_im": p["lam_im"][i][:, :, None],
        "log_dt": p["log_dt"][i][:, None, None],
        "b_re": p["s5_b_re"][i],
        "b_im": p["s5_b_im"][i],
        "bx_re": jnp.tile(p["s5_b_re"][i], (1, 1, S5_T)),
        "bx_im": jnp.tile(p["s5_b_im"][i], (1, 1, S5_T)),
        "cx_re": jnp.tile(jnp.swapaxes(p["s5_c_re"][i], 1, 2), (1, 1, S5_T)),
        "cx_im": jnp.tile(jnp.swapaxes(p["s5_c_im"][i], 1, 2), (1, 1, S5_T)),
        "s5_d": p["s5_d"][i][:, :, None],
        "w_glu": p["w_glu"][i].astype(BF16),
        "b_glu": p["b_glu"][i][None],
        "w_out": p["w_out"][i].astype(BF16),
    }
    return out


def _moe_layer(hn, x_mid, logits_t, g3, wg, wu, wd):
    n = hn.shape[0]
    n_tiles = 2 * n // MOE_TM + N_EXPERTS
    slots, gates, meta = _router(logits_t, MOE_TM)
    xs = _dispatch(hn, slots, n_tiles * MOE_TM)
    ys = _moe_ffn(xs, meta[0, :n_tiles], meta[1, :1], wg, wu, wd)
    return _combine(ys, slots, gates, x_mid, g3)


def kernel(x, ln_gains, w_in, conv_w, a_log, dt_bias, a_norm, sinks, b_norm, lam_re, lam_im, log_dt,
           s5_b_re, s5_b_im, s5_c_re, s5_c_im, s5_d, w_glu, b_glu, c_norm, w_out, ffn_wg, ffn_wu, ffn_wd,
           w_router, moe_wg, moe_wu, moe_wd):
    p = dict(w_in=w_in, a_log=a_log, dt_bias=dt_bias, sinks=sinks, lam_re=lam_re, lam_im=lam_im, log_dt=log_dt,
             s5_b_re=s5_b_re, s5_b_im=s5_b_im, s5_c_re=s5_c_re, s5_c_im=s5_c_im, s5_d=s5_d, w_glu=w_glu,
             b_glu=b_glu, w_out=w_out)
    bsz, seq, d = x.shape
    n = bsz * seq
    xf = x.reshape(n, d)
    for i in range(ln_gains.shape[0]):
        prm = _prep_layer(p, i)
        g = ln_gains[i]
        main, ab = _inproj(xf, g[0:1], prm["w_main"], prm["w_ab"])
        proj3 = main.reshape(bsz, seq, N_MAIN)
        ya = _deltanet(proj3, ab.reshape(bsz, seq, LANES), conv_w[i], prm["alog_row"], prm["dtb_row"],
                       a_norm[i][None]).reshape(n, A_WIDTH)
        yb = _swa(proj3, prm["sinks_row"], b_norm[i][None]).reshape(n, B_WIDTH)
        yc = _s5(proj3, prm).reshape(n, C_WIDTH)
        j = i // 2
        post_args = (ya, yb, yc, prm["w_glu"], prm["b_glu"], c_norm[i][None], prm["w_out"], xf, g[1:2], g[2:3])
        if i % 2 == 0:
            x_mid, hn = _post(*post_args)
            xf = _ffn(hn, ffn_wg[j].astype(BF16), ffn_wu[j].astype(BF16), ffn_wd[j].astype(BF16), x_mid, g[3:4])
        else:
            x_mid, hn, logits_t = _post(*post_args, w_rt=jnp.transpose(w_router[j]))
            xf = _moe_layer(hn, x_mid, logits_t, g[3:4], moe_wg[j].astype(BF16), moe_wu[j].astype(BF16),
                            moe_wd[j].astype(BF16))
    return xf.reshape(bsz, seq, d)
```
